```python
import math
import jax
import jax.numpy as jnp
from jax import lax
import numpy as np

D_MODEL = 4096
BATCH = 1
SEQ = 8192
DEPTH = 2

N_EVEN = (DEPTH + 1) // 2
N_ODD = DEPTH // 2
MIX_WIDTH = D_MODEL
GROUP_WIDTH = MIX_WIDTH // 2

MLA_HEADS = 16
MLA_NOPE_DIM = 128
MLA_ROPE_DIM = 64
MLA_V_DIM = 128
Q_LORA_RANK = 768
KV_LORA_RANK = 512
SC_WIDTH = GROUP_WIDTH
SC_KERNEL = 3
DIFF_HEADS = 8
DIFF_QK_DIM = 128
DIFF_V_DIM = 2 * DIFF_QK_DIM
DIFF_ROT_DIM = DIFF_QK_DIM // 4
CF_WIDTH = GROUP_WIDTH
CF_KERNEL = 31
N_EXPERTS = 32
TOP_K = 4
EXPERT_FF = 768
SWIGLU_LIMIT = 7.0
SWIGLU_ALPHA = 1.702
MOE_BLOCK = 128
ROPE_THETA = 500000.0
Q_BLOCK = 128
NORM_EPS = 1e-5
RMS_EPS = 1e-6
DEEPNORM_ALPHA = (2.0 * DEPTH) ** 0.25
DEEPNORM_BETA = (8.0 * DEPTH) ** -0.25

EVEN_IN_WIDTH = Q_LORA_RANK + KV_LORA_RANK + MLA_ROPE_DIM + 3 * SC_WIDTH
ODD_IN_WIDTH = 2 * DIFF_HEADS * 2 * DIFF_QK_DIM + DIFF_HEADS * DIFF_V_DIM + 2 * CF_WIDTH

kernel_name = 'hybrid_mla_shortconv_conformer_diffattn_moe'


def lambda_init(layer):
    return 0.8 - 0.6 * math.exp(-0.3 * layer)


def layer_norm(x, g, b):
    xf = x.astype(jnp.float32)
    mu = jnp.mean(xf, -1, keepdims=True)
    var = jnp.mean(jnp.square(xf - mu), -1, keepdims=True)
    y = (xf - mu) * lax.rsqrt(var + NORM_EPS)
    return (y * g.astype(jnp.float32) + b.astype(jnp.float32)).astype(x.dtype)


def rms_norm(x, g):
    xf = x.astype(jnp.float32)
    y = xf * lax.rsqrt(jnp.mean(jnp.square(xf), -1, keepdims=True) + RMS_EPS)
    return (y * g.astype(jnp.float32)).astype(x.dtype)


def rope_tables(positions, dim):
    inv_freq = ROPE_THETA ** (-jnp.arange(0, dim, 2, dtype=jnp.float32) / dim)
    ang = positions.astype(jnp.float32)[..., None] * inv_freq
    return jnp.cos(ang), jnp.sin(ang)


def apply_rope(x, cos, sin):
    xf = x.astype(jnp.float32)
    x1, x2 = jnp.split(xf, 2, axis=-1)
    c = cos[:, :, None, :]
    s = sin[:, :, None, :]
    return jnp.concatenate([x1 * c - x2 * s, x2 * c + x1 * s], -1).astype(x.dtype)


def partial_rope(x, cos, sin, rot_dim):
    return jnp.concatenate([apply_rope(x[..., :rot_dim], cos, sin), x[..., rot_dim:]], -1)


def causal_depthwise_conv(z, w):
    k, c = w.shape
    return lax.conv_general_dilated(
        z, w[:, None, :].astype(z.dtype), window_strides=(1,), padding=[(k - 1, 0)],
        dimension_numbers=('NWC', 'WIO', 'NWC'), feature_group_count=c)


def causal_softmax(scores, start):
    q_pos = start + jnp.arange(Q_BLOCK)
    k_pos = jnp.arange(scores.shape[-1])
    allowed = k_pos[None, :] <= q_pos[:, None]
    return jax.nn.softmax(jnp.where(allowed, scores, -jnp.inf), axis=-1)


def sweep_query_blocks(block_fn, seq):
    out = lax.map(block_fn, jnp.arange(seq // Q_BLOCK))
    nb, b, qb, h, d = out.shape
    return jnp.moveaxis(out, 0, 1).reshape(b, nb * qb, h, d)


def mla_mixer(h_q_a, h_kv_a, h_k_pe, cos, sin, q_a_norm, w_q_b, kv_a_norm, w_kv_b):
    b, s, _ = h_q_a.shape
    q = (rms_norm(h_q_a, q_a_norm) @ w_q_b).reshape(b, s, MLA_HEADS, MLA_NOPE_DIM + MLA_ROPE_DIM)
    q_nope = q[..., :MLA_NOPE_DIM]
    q_pe = apply_rope(q[..., MLA_NOPE_DIM:], cos, sin)
    kv = (rms_norm(h_kv_a, kv_a_norm) @ w_kv_b).reshape(b, s, MLA_HEADS, MLA_NOPE_DIM + MLA_V_DIM)
    k_nope, v = kv[..., :MLA_NOPE_DIM], kv[..., MLA_NOPE_DIM:]
    k_pe = apply_rope(h_k_pe[:, :, None, :], cos, sin)[:, :, 0, :]
    scale = (MLA_NOPE_DIM + MLA_ROPE_DIM) ** -0.5

    def block(i):
        start = i * Q_BLOCK
        qn = lax.dynamic_slice_in_dim(q_nope, start, Q_BLOCK, axis=1)
        qp = lax.dynamic_slice_in_dim(q_pe, start, Q_BLOCK, axis=1)
        sc = (jnp.einsum('bqhd,bkhd->bhqk', qn, k_nope, preferred_element_type=jnp.float32)
              + jnp.einsum('bqhr,bkr->bhqk', qp, k_pe, preferred_element_type=jnp.float32)) * scale
        p = causal_softmax(sc, start)
        return jnp.einsum('bhqk,bkhd->bqhd', p.astype(v.dtype), v)

    return sweep_query_blocks(block, s).reshape(b, s, MLA_HEADS * MLA_V_DIM)


def short_conv_mixer(gate_b, gate_c, u, conv_w):
    return gate_b * causal_depthwise_conv(gate_c * u, conv_w)


def diff_attention(h_q, h_k, h_v, cos, sin, lq1, lk1, lq2, lk2, subln_g, lam_init):
    b, s, _ = h_q.shape
    q = partial_rope(h_q.reshape(b, s, DIFF_HEADS * 2, DIFF_QK_DIM), cos, sin, DIFF_ROT_DIM)
    k = partial_rope(h_k.reshape(b, s, DIFF_HEADS * 2, DIFF_QK_DIM), cos, sin, DIFF_ROT_DIM)
    q = q.reshape(b, s, DIFF_HEADS, 2, DIFF_QK_DIM)
    k = k.reshape(b, s, DIFF_HEADS, 2, DIFF_QK_DIM)
    q1, q2 = q[:, :, :, 0], q[:, :, :, 1]
    k1, k2 = k[:, :, :, 0], k[:, :, :, 1]
    v = h_v.reshape(b, s, DIFF_HEADS, DIFF_V_DIM)
    lam = (jnp.exp(jnp.sum(lq1.astype(jnp.float32) * lk1.astype(jnp.float32)))
           - jnp.exp(jnp.sum(lq2.astype(jnp.float32) * lk2.astype(jnp.float32))) + lam_init)
    scale = DIFF_QK_DIM ** -0.5

    def block(i):
        start = i * Q_BLOCK
        q1b = lax.dynamic_slice_in_dim(q1, start, Q_BLOCK, axis=1)
        q2b = lax.dynamic_slice_in_dim(q2, start, Q_BLOCK, axis=1)
        p1 = causal_softmax(jnp.einsum('bqhd,bkhd->bhqk', q1b, k1, preferred_element_type=jnp.float32) * scale, start)
        p2 = causal_softmax(jnp.einsum('bqhd,bkhd->bhqk', q2b, k2, preferred_element_type=jnp.float32) * scale, start)
        a = p1 - lam * p2
        return jnp.einsum('bhqk,bkhd->bqhd', a.astype(v.dtype), v)

    o = sweep_query_blocks(block, s)
    o = rms_norm(o, subln_g) * (1.0 - lam_init)
    return o.reshape(b, s, DIFF_HEADS * DIFF_V_DIM)


def conformer_conv_mixer(h, conv_w, conv_b, norm_g, norm_b):
    a, g = jnp.split(h, 2, axis=-1)
    z = a * jax.nn.sigmoid(g)
    z = causal_depthwise_conv(z, conv_w) + conv_b
    return jax.nn.silu(layer_norm(z, norm_g, norm_b))


def even_mixer(x, cos, sin, w_in, q_a_norm, w_q_b, kv_a_norm, w_kv_b, conv_w, w_out):
    h = x @ w_in
    cuts = [int(c) for c in np.cumsum([Q_LORA_RANK, KV_LORA_RANK, MLA_ROPE_DIM, SC_WIDTH, SC_WIDTH])]
    h_q_a, h_kv_a, h_k_pe, sc_b, sc_c, sc_u = jnp.split(h, cuts, axis=-1)
    attn = mla_mixer(h_q_a, h_kv_a, h_k_pe, cos, sin, q_a_norm, w_q_b, kv_a_norm, w_kv_b)
    conv = short_conv_mixer(sc_b, sc_c, sc_u, conv_w)
    return jnp.concatenate([attn, conv], -1) @ w_out


def odd_mixer(x, cos, sin, w_in, cf_conv_w, cf_conv_b, cf_norm_g, cf_norm_b,
              lq1, lk1, lq2, lk2, subln_g, w_out, lam_init):
    h = x @ w_in
    qk = DIFF_HEADS * 2 * DIFF_QK_DIM
    h_q, h_k, h_v, h_cf = jnp.split(h, [qk, 2 * qk, 2 * qk + DIFF_HEADS * DIFF_V_DIM], axis=-1)
    attn = diff_attention(h_q, h_k, h_v, cos, sin, lq1, lk1, lq2, lk2, subln_g, lam_init)
    conv = conformer_conv_mixer(h_cf, cf_conv_w, cf_conv_b, cf_norm_g, cf_norm_b)
    return jnp.concatenate([attn, conv], -1) @ w_out


def moe_ffn(x, router_w, router_b, w_gate_up, b_gate_up, w_down, b_down):
    b, s, d = x.shape
    xt = x.reshape(b * s, d)
    n = xt.shape[0]
    n_assign = n * TOP_K
    logits = jnp.matmul(xt, router_w, preferred_element_type=jnp.float32) + router_b.astype(jnp.float32)
    top_logit, top_idx = lax.top_k(logits, TOP_K)
    gates = jax.nn.softmax(top_logit, axis=-1)
    flat_e = top_idx.reshape(-1)
    flat_tok = jnp.repeat(jnp.arange(n, dtype=jnp.int32), TOP_K)
    flat_gate = gates.reshape(-1)
    order = jnp.argsort(flat_e)
    e_sorted = flat_e[order]
    counts = jnp.bincount(flat_e, length=N_EXPERTS)
    padded = (counts + MOE_BLOCK - 1) // MOE_BLOCK * MOE_BLOCK
    start = jnp.cumsum(counts) - counts
    pend = jnp.cumsum(padded)
    pstart = pend - padded
    dest = pstart[e_sorted] + (jnp.arange(n_assign) - start[e_sorted])
    n_blocks = -(-n_assign // MOE_BLOCK) + N_EXPERTS
    cap = n_blocks * MOE_BLOCK
    buf_tok = jnp.zeros((cap,), jnp.int32).at[dest].set(flat_tok[order])
    buf_gate = jnp.zeros((cap,), jnp.float32).at[dest].set(flat_gate[order])
    block_expert = jnp.minimum(
        jnp.searchsorted(pend, jnp.arange(n_blocks) * MOE_BLOCK, side='right'), N_EXPERTS - 1)

    def expert_block(args):
        tok, gate, e = args
        h = xt[tok] @ w_gate_up[e] + b_gate_up[e]
        glu, lin = jnp.split(h, 2, axis=-1)
        glu = jnp.minimum(glu, SWIGLU_LIMIT)
        lin = jnp.clip(lin, -SWIGLU_LIMIT, SWIGLU_LIMIT)
        act = glu * jax.nn.sigmoid(SWIGLU_ALPHA * glu) * (lin + 1.0)
        out = act @ w_down[e] + b_down[e]
        return out * gate[:, None].astype(out.dtype)

    outs = lax.map(expert_block, (buf_tok.reshape(n_blocks, MOE_BLOCK),
                                  buf_gate.reshape(n_blocks, MOE_BLOCK), block_expert))
    y = jnp.zeros_like(xt).at[buf_tok].add(outs.reshape(cap, d))
    return y.reshape(b, s, d)


def setup_inputs(seed: int = 0) -> dict:
    key = jax.random.key(seed)
    ks = iter(jax.random.split(key, 40))

    def nrm(shape, scale):
        return jax.random.normal(next(ks), shape, jnp.float32) * scale

    def gain(shape):
        return 1.0 + nrm(shape, 0.01)

    return {
        'x': nrm((BATCH, SEQ, D_MODEL), 1.0),
        'positions': jnp.broadcast_to(jnp.arange(SEQ, dtype=jnp.int32), (BATCH, SEQ)),
        'e_w_in': nrm((N_EVEN, D_MODEL, EVEN_IN_WIDTH), D_MODEL ** -0.5),
        'e_q_a_norm': gain((N_EVEN, Q_LORA_RANK)),
        'e_w_q_b': nrm((N_EVEN, Q_LORA_RANK, MLA_HEADS * (MLA_NOPE_DIM + MLA_ROPE_DIM)), Q_LORA_RANK ** -0.5),
        'e_kv_a_norm': gain((N_EVEN, KV_LORA_RANK)),
        'e_w_kv_b': nrm((N_EVEN, KV_LORA_RANK, MLA_HEADS * (MLA_NOPE_DIM + MLA_V_DIM)), KV_LORA_RANK ** -0.5),
        'e_conv_w': nrm((N_EVEN, SC_KERNEL, SC_WIDTH), SC_KERNEL ** -0.5),
        'e_w_out': nrm((N_EVEN, MIX_WIDTH, D_MODEL), DEEPNORM_BETA * MIX_WIDTH ** -0.5),
        'o_w_in': nrm((N_ODD, D_MODEL, ODD_IN_WIDTH), D_MODEL ** -0.5),
        'o_cf_conv_w': nrm((N_ODD, CF_KERNEL, CF_WIDTH), CF_KERNEL ** -0.5),
        'o_cf_conv_b': nrm((N_ODD, CF_WIDTH), 0.01),
        'o_cf_norm_g': gain((N_ODD, CF_WIDTH)),
        'o_cf_norm_b': nrm((N_ODD, CF_WIDTH), 0.01),
        'o_lambda_q1': nrm((N_ODD, DIFF_QK_DIM), 0.1),
        'o_lambda_k1': nrm((N_ODD, DIFF_QK_DIM), 0.1),
        'o_lambda_q2': nrm((N_ODD, DIFF_QK_DIM), 0.1),
        'o_lambda_k2': nrm((N_ODD, DIFF_QK_DIM), 0.1),
        'o_subln_g': gain((N_ODD, DIFF_V_DIM)),
        'o_w_out': nrm((N_ODD, MIX_WIDTH, D_MODEL), DEEPNORM_BETA * MIX_WIDTH ** -0.5),
        'ln_mix_g': gain((DEPTH, D_MODEL)),
        'ln_mix_b': nrm((DEPTH, D_MODEL), 0.01),
        'router_w': nrm((DEPTH, D_MODEL, N_EXPERTS), D_MODEL ** -0.5),
        'router_b': nrm((DEPTH, N_EXPERTS), 0.01),
        'moe_w_gate_up': nrm((DEPTH, N_EXPERTS, D_MODEL, 2 * EXPERT_FF), D_MODEL ** -0.5),
        'moe_b_gate_up': nrm((DEPTH, N_EXPERTS, 2 * EXPERT_FF), 0.01),
        'moe_w_down': nrm((DEPTH, N_EXPERTS, EXPERT_FF, D_MODEL), DEEPNORM_BETA * EXPERT_FF ** -0.5),
        'moe_b_down': nrm((DEPTH, N_EXPERTS, D_MODEL), 0.01),
        'ln_ffn_g': gain((DEPTH, D_MODEL)),
        'ln_ffn_b': nrm((DEPTH, D_MODEL), 0.01),
    }


def reference(x, positions, e_w_in, e_q_a_norm, e_w_q_b, e_kv_a_norm, e_w_kv_b, e_conv_w, e_w_out,
              o_w_in, o_cf_conv_w, o_cf_conv_b, o_cf_norm_g, o_cf_norm_b,
              o_lambda_q1, o_lambda_k1, o_lambda_q2, o_lambda_k2, o_subln_g, o_w_out,
              ln_mix_g, ln_mix_b, router_w, router_b, moe_w_gate_up, moe_b_gate_up,
              moe_w_down, moe_b_down, ln_ffn_g, ln_ffn_b):
    cos_mla, sin_mla = rope_tables(positions, MLA_ROPE_DIM)
    cos_d, sin_d = rope_tables(positions, DIFF_ROT_DIM)
    for layer in range(DEPTH):
        i = layer // 2
        if layer % 2 == 0:
            mix = even_mixer(x, cos_mla, sin_mla, e_w_in[i], e_q_a_norm[i], e_w_q_b[i],
                             e_kv_a_norm[i], e_w_kv_b[i], e_conv_w[i], e_w_out[i])
        else:
            mix = odd_mixer(x, cos_d, sin_d, o_w_in[i], o_cf_conv_w[i], o_cf_conv_b[i],
                            o_cf_norm_g[i], o_cf_norm_b[i], o_lambda_q1[i], o_lambda_k1[i],
                            o_lambda_q2[i], o_lambda_k2[i], o_subln_g[i], o_w_out[i],
                            lambda_init(layer))
        x = layer_norm(DEEPNORM_ALPHA * x + mix, ln_mix_g[layer], ln_mix_b[layer])
        ffn = moe_ffn(x, router_w[layer], router_b[layer], moe_w_gate_up[layer],
                      moe_b_gate_up[layer], moe_w_down[layer], moe_b_down[layer])
        x = layer_norm(DEEPNORM_ALPHA * x + ffn, ln_ffn_g[layer], ln_ffn_b[layer])
    return x
```

```python
import functools
import math

import jax
import jax.numpy as jnp
from jax import lax
from jax.experimental import pallas as pl
from jax.experimental.pallas import tpu as pltpu

BF16 = jnp.bfloat16
F32 = jnp.float32

DEPTH = 2
MLA_HEADS = 16
MLA_NOPE_DIM = 128
MLA_ROPE_DIM = 64
MLA_V_DIM = 128
Q_LORA_RANK = 768
KV_LORA_RANK = 512
SC_KERNEL = 3
DIFF_HEADS = 8
DIFF_QK_DIM = 128
DIFF_V_DIM = 256
DIFF_ROT_DIM = 32
CF_KERNEL = 31
N_EXPERTS = 32
TOP_K = 4
EXPERT_FF = 768
SWIGLU_LIMIT = 7.0
SWIGLU_ALPHA = 1.702
ROPE_THETA = 500000.0
NORM_EPS = 1e-5
RMS_EPS = 1e-6
DEEPNORM_ALPHA = (2.0 * DEPTH) ** 0.25

V7X_LANES = 128
V7X_SUBLANES = 8
V7X_VMEM_BYTES = 64 * 1024 * 1024
VMEM_LIMIT_BYTES = (V7X_VMEM_BYTES * 7) // 8

MOE_ROWS = 256
ROPE_HALF_ROLL = 64


def _cparams(semantics, vmem=VMEM_LIMIT_BYTES):
    return pltpu.CompilerParams(dimension_semantics=semantics, vmem_limit_bytes=vmem)


def _bdot(a, b):
    return jnp.dot(a, b, preferred_element_type=F32)


def _dot_nt(a, b):
    return lax.dot_general(a, b, (((1,), (1,)), ((), ())), preferred_element_type=F32)


def _mm_kernel(a_ref, b_ref, o_ref):
    o_ref[...] = _bdot(a_ref[...], b_ref[...]).astype(o_ref.dtype)


def matmul(a, b, out_dtype, tm, tn):
    m, k = a.shape
    _, n = b.shape
    return pl.pallas_call(
        _mm_kernel,
        grid=(m // tm, n // tn),
        in_specs=[pl.BlockSpec((tm, k), lambda i, j: (i, 0)),
                  pl.BlockSpec((k, tn), lambda i, j: (0, j))],
        out_specs=pl.BlockSpec((tm, tn), lambda i, j: (i, j)),
        out_shape=jax.ShapeDtypeStruct((m, n), out_dtype),
        compiler_params=_cparams(("parallel", "parallel")),
        name="matmul",
    )(a, b)


def _rope_pairs(t, cs):
    u = t * cs
    r = u + pltpu.roll(u, ROPE_HALF_ROLL, 1)
    lane = lax.broadcasted_iota(jnp.int32, r.shape, 1)
    return jnp.where(lane < ROPE_HALF_ROLL, r, 0.0)


def _even_a_kernel(x_ref, w_ref, gq_ref, gkv_ref, cs_ref, qn_ref, kvn_ref, kpe_ref):
    acc = _bdot(x_ref[...], w_ref[...])
    hq = acc[:, :Q_LORA_RANK]
    hkv = acc[:, Q_LORA_RANK:Q_LORA_RANK + KV_LORA_RANK]
    hpe = acc[:, Q_LORA_RANK + KV_LORA_RANK:]
    qn = hq * lax.rsqrt(jnp.mean(hq * hq, -1, keepdims=True) + RMS_EPS) * gq_ref[...]
    kvn = hkv * lax.rsqrt(jnp.mean(hkv * hkv, -1, keepdims=True) + RMS_EPS) * gkv_ref[...]
    qn_ref[...] = qn.astype(BF16)
    kvn_ref[...] = kvn.astype(BF16)
    kpe_ref[...] = _rope_pairs(hpe, cs_ref[...]).astype(BF16)


def even_a(xb, w_a, gq, gkv, cs, tm):
    s, d = xb.shape
    n = w_a.shape[1]
    return pl.pallas_call(
        _even_a_kernel,
        grid=(s // tm,),
        in_specs=[pl.BlockSpec((tm, d), lambda i: (i, 0)),
                  pl.BlockSpec((d, n), lambda i: (0, 0)),
                  pl.BlockSpec((1, Q_LORA_RANK), lambda i: (0, 0)),
                  pl.BlockSpec((1, KV_LORA_RANK), lambda i: (0, 0)),
                  pl.BlockSpec((tm, V7X_LANES), lambda i: (i, 0))],
        out_specs=[pl.BlockSpec((tm, Q_LORA_RANK), lambda i: (i, 0)),
                   pl.BlockSpec((tm, KV_LORA_RANK), lambda i: (i, 0)),
                   pl.BlockSpec((tm, V7X_LANES), lambda i: (i, 0))],
        out_shape=[jax.ShapeDtypeStruct((s, Q_LORA_RANK), BF16),
                   jax.ShapeDtypeStruct((s, KV_LORA_RANK), BF16),
                   jax.ShapeDtypeStruct((s, V7X_LANES), BF16)],
        compiler_params=_cparams(("parallel",)),
        name="even_a",
    )(xb, w_a, gq, gkv, cs)


def _mla_q_kernel(a_ref, w_ref, cs_ref, o_ref, *, heads, scale):
    acc = _bdot(a_ref[...], w_ref[...])
    cs = cs_ref[...]
    for hh in range(heads):
        base = hh * 2 * V7X_LANES
        nope = acc[:, base:base + V7X_LANES]
        rot = _rope_pairs(acc[:, base + V7X_LANES:base + 2 * V7X_LANES], cs)
        o_ref[hh] = (jnp.concatenate([nope, rot], axis=1) * scale).astype(BF16)


def mla_q(qn, w_q, cs, tm, heads_per_step):
    s, k = qn.shape
    n = w_q.shape[1]
    hd = 2 * V7X_LANES
    tn = heads_per_step * hd
    return pl.pallas_call(
        functools.partial(_mla_q_kernel, heads=heads_per_step,
                          scale=(MLA_NOPE_DIM + MLA_ROPE_DIM) ** -0.5),
        grid=(s // tm, n // tn),
        in_specs=[pl.BlockSpec((tm, k), lambda i, j: (i, 0)),
                  pl.BlockSpec((k, tn), lambda i, j: (0, j)),
                  pl.BlockSpec((tm, V7X_LANES), lambda i, j: (i, 0))],
        out_specs=pl.BlockSpec((heads_per_step, tm, hd), lambda i, j: (j, i, 0)),
        out_shape=jax.ShapeDtypeStruct((n // hd, s, hd), BF16),
        compiler_params=_cparams(("parallel", "parallel")),
        name="mla_q",
    )(qn, w_q, cs)


def _mla_k_kernel(a_ref, w_ref, kpe_ref, o_ref, *, heads):
    acc = _bdot(a_ref[...], w_ref[...])
    kpe = kpe_ref[...]
    for hh in range(heads):
        nope = acc[:, hh * V7X_LANES:(hh + 1) * V7X_LANES].astype(BF16)
        o_ref[hh] = jnp.concatenate([nope, kpe], axis=1)


def mla_k(kvn, w_k, kpe, tm, heads_per_step):
    s, k = kvn.shape
    n = w_k.shape[1]
    tn = heads_per_step * V7X_LANES
    return pl.pallas_call(
        functools.partial(_mla_k_kernel, heads=heads_per_step),
        grid=(s // tm, n // tn),
        in_specs=[pl.BlockSpec((tm, k), lambda i, j: (i, 0)),
                  pl.BlockSpec((k, tn), lambda i, j: (0, j)),
                  pl.BlockSpec((tm, V7X_LANES), lambda i, j: (i, 0))],
        out_specs=pl.BlockSpec((heads_per_step, tm, 2 * V7X_LANES), lambda i, j: (j, i, 0)),
        out_shape=jax.ShapeDtypeStruct((n // V7X_LANES, s, 2 * V7X_LANES), BF16),
        compiler_params=_cparams(("parallel", "parallel")),
        name="mla_k",
    )(kvn, w_k, kpe)


def _flash_sub(q, k_ref, v_ref, sub, qi, tq, tk, m_sc, l_sc, acc_sc):
    m_sc[...] = jnp.full(m_sc.shape, -jnp.inf, F32)
    l_sc[...] = jnp.zeros(l_sc.shape, F32)
    acc_sc[...] = jnp.zeros(acc_sc.shape, F32)

    def step(j, masked):
        start = pl.multiple_of(j * tk, tk)
        k = k_ref[sub, pl.ds(start, tk), :]
        s = _dot_nt(q, k)
        if masked:
            row = qi * tq + lax.broadcasted_iota(jnp.int32, s.shape, 0)
            col = start + lax.broadcasted_iota(jnp.int32, s.shape, 1)
            s = jnp.where(col <= row, s, -jnp.inf)
        m_prev = m_sc[...]
        m_new = jnp.maximum(m_prev, jnp.max(s, axis=1, keepdims=True))
        alpha = jnp.exp(m_prev - m_new)
        p = jnp.exp(s - m_new)
        l_sc[...] = alpha * l_sc[...] + jnp.sum(p, axis=1, keepdims=True)
        acc_sc[...] = alpha * acc_sc[...] + _bdot(p.astype(BF16), v_ref[pl.ds(start, tk), :])
        m_sc[...] = m_new

    n_full = qi * (tq // tk)

    def body(j, carry):
        step(j, False)
        return carry

    lax.fori_loop(0, n_full, body, 0)
    for d in range(tq // tk):
        step(n_full + d, True)
    return acc_sc[...] / l_sc[...]


def _mla_flash_kernel(q_ref, k_ref, v_ref, o_ref, m_sc, l_sc, acc_sc, *, tq, tk):
    qi = pl.program_id(1)
    o = _flash_sub(q_ref[0], k_ref, v_ref, 0, qi, tq, tk, m_sc, l_sc, acc_sc)
    o_ref[...] = o.astype(o_ref.dtype)


def mla_flash(q_cat, k_cat, v, tq, tk):
    h, s, dq = q_cat.shape
    dv = MLA_V_DIM
    return pl.pallas_call(
        functools.partial(_mla_flash_kernel, tq=tq, tk=tk),
        grid=(h, s // tq),
        in_specs=[pl.BlockSpec((1, tq, dq), lambda hh, i: (hh, i, 0)),
                  pl.BlockSpec((1, s, dq), lambda hh, i: (hh, 0, 0)),
                  pl.BlockSpec((s, dv), lambda hh, i: (0, hh))],
        out_specs=pl.BlockSpec((tq, dv), lambda hh, i: (i, hh)),
        out_shape=jax.ShapeDtypeStruct((s, h * dv), BF16),
        scratch_shapes=[pltpu.VMEM((tq, 1), F32), pltpu.VMEM((tq, 1), F32),
                        pltpu.VMEM((tq, dv), F32)],
        compiler_params=_cparams(("parallel", "arbitrary")),
        name="mla_flash",
    )(q_cat, k_cat, v)


def _diff_flash_kernel(q_ref, k_ref, v_ref, lq1_ref, lk1_ref, lq2_ref, lk2_ref, g_ref,
                       o_ref, m_sc, l_sc, acc_sc, *, tq, tk, lam_init):
    qi = pl.program_id(1)
    o1 = _flash_sub(q_ref[0], k_ref, v_ref, 0, qi, tq, tk, m_sc, l_sc, acc_sc)
    o2 = _flash_sub(q_ref[1], k_ref, v_ref, 1, qi, tq, tk, m_sc, l_sc, acc_sc)
    lam = (jnp.exp(jnp.sum(lq1_ref[...] * lk1_ref[...], keepdims=True))
           - jnp.exp(jnp.sum(lq2_ref[...] * lk2_ref[...], keepdims=True)) + lam_init)
    o = o1 - lam * o2
    o = o * lax.rsqrt(jnp.mean(o * o, -1, keepdims=True) + RMS_EPS) * g_ref[...]
    o_ref[...] = (o * (1.0 - lam_init)).astype(o_ref.dtype)


def diff_flash(qk_heads, v, lq1, lk1, lq2, lk2, subln_g, lam_init, tq, tk):
    nh, s, dq = qk_heads.shape
    h = nh // 4
    dv = DIFF_V_DIM
    vec = pl.BlockSpec((1, dq), lambda hh, i: (0, 0))
    return pl.pallas_call(
        functools.partial(_diff_flash_kernel, tq=tq, tk=tk, lam_init=lam_init),
        grid=(h, s // tq),
        in_specs=[pl.BlockSpec((2, tq, dq), lambda hh, i: (hh, i, 0)),
                  pl.BlockSpec((2, s, dq), lambda hh, i: (h + hh, 0, 0)),
                  pl.BlockSpec((s, dv), lambda hh, i: (0, hh)),
                  vec, vec, vec, vec,
                  pl.BlockSpec((1, dv), lambda hh, i: (0, 0))],
        out_specs=pl.BlockSpec((tq, dv), lambda hh, i: (i, hh)),
        out_shape=jax.ShapeDtypeStruct((s, h * dv), BF16),
        scratch_shapes=[pltpu.VMEM((tq, 1), F32), pltpu.VMEM((tq, 1), F32),
                        pltpu.VMEM((tq, dv), F32)],
        compiler_params=_cparams(("parallel", "arbitrary")),
        name="diff_flash",
    )(qk_heads, qk_heads, v, lq1, lk1, lq2, lk2, subln_g)


def _short_conv_kernel(b_ref, c_ref, u_ref, ch_ref, uh_ref, w_ref, o_ref, z_sc, *, tm):
    i = pl.program_id(0)
    halo = ch_ref[...] * uh_ref[...]
    z_sc[0:V7X_SUBLANES, :] = jnp.where(i > 0, halo, 0.0)
    z_sc[V7X_SUBLANES:, :] = c_ref[...] * u_ref[...]
    acc = None
    for j in range(SC_KERNEL):
        off = V7X_SUBLANES - (SC_KERNEL - 1) + j
        term = w_ref[j:j + 1, :] * z_sc[off:off + tm, :]
        acc = term if acc is None else acc + term
    o_ref[...] = (b_ref[...] * acc).astype(o_ref.dtype)


def short_conv(hsc, conv_w, tm):
    s, three_c = hsc.shape
    c = three_c // 3
    hb = tm // V7X_SUBLANES
    return pl.pallas_call(
        functools.partial(_short_conv_kernel, tm=tm),
        grid=(s // tm,),
        in_specs=[pl.BlockSpec((tm, c), lambda i: (i, 0)),
                  pl.BlockSpec((tm, c), lambda i: (i, 1)),
                  pl.BlockSpec((tm, c), lambda i: (i, 2)),
                  pl.BlockSpec((V7X_SUBLANES, c), lambda i: (jnp.maximum(i * hb - 1, 0), 1)),
                  pl.BlockSpec((V7X_SUBLANES, c), lambda i: (jnp.maximum(i * hb - 1, 0), 2)),
                  pl.BlockSpec((SC_KERNEL, c), lambda i: (0, 0))],
        out_specs=pl.BlockSpec((tm, c), lambda i: (i, 0)),
        out_shape=jax.ShapeDtypeStruct((s, c), BF16),
        scratch_shapes=[pltpu.VMEM((tm + V7X_SUBLANES, c), F32)],
        compiler_params=_cparams(("parallel",)),
        name="short_conv",
    )(hsc, hsc, hsc, hsc, hsc, conv_w)


CF_HALO = 32


CF_ROWS = 128


def _conformer_kernel(a_ref, g_ref, ah_ref, gh_ref, w_ref, cb_ref, ng_ref, nb_ref, o_ref,
                      z_sc, y_sc, *, tm, chunks):
    i = pl.program_id(0)
    c = pl.program_id(1)
    halo = ah_ref[...] * jax.nn.sigmoid(gh_ref[...])
    z_sc[0:CF_HALO, :] = jnp.where(i > 0, halo, 0.0)
    z_sc[CF_HALO:, :] = a_ref[...] * jax.nn.sigmoid(g_ref[...])
    w = w_ref[0]
    for r0 in range(0, tm, CF_ROWS):
        acc = jnp.zeros((CF_ROWS, V7X_LANES), F32)
        for j in range(CF_KERNEL):
            off = r0 + CF_HALO - (CF_KERNEL - 1) + j
            acc = acc + w[j:j + 1, :] * z_sc[off:off + CF_ROWS, :]
        y_sc[c, r0:r0 + CF_ROWS, :] = acc + cb_ref[0]

    @pl.when(c == chunks - 1)
    def _():
        total = jnp.zeros((tm, 1), F32)
        for k in range(chunks):
            total = total + jnp.sum(y_sc[k], axis=-1, keepdims=True)
        mu = total / (chunks * V7X_LANES)
        sq = jnp.zeros((tm, 1), F32)
        for k in range(chunks):
            zc = y_sc[k] - mu
            sq = sq + jnp.sum(zc * zc, axis=-1, keepdims=True)
        inv = lax.rsqrt(sq / (chunks * V7X_LANES) + NORM_EPS)
        for k in range(chunks):
            lanes = slice(k * V7X_LANES, (k + 1) * V7X_LANES)
            y = (y_sc[k] - mu) * inv * ng_ref[:, lanes] + nb_ref[:, lanes]
            o_ref[:, lanes] = (y * jax.nn.sigmoid(y)).astype(o_ref.dtype)


def conformer_conv(hcf, conv_w, conv_b, norm_g, norm_b, tm):
    s, two_c = hcf.shape
    c = two_c // 2
    chunks = c // V7X_LANES
    hb = tm // CF_HALO
    w3 = conv_w.reshape(CF_KERNEL, chunks, V7X_LANES).transpose(1, 0, 2)
    cb3 = conv_b.reshape(chunks, 1, V7X_LANES)
    vec = pl.BlockSpec((1, c), lambda i, k: (0, 0))
    return pl.pallas_call(
        functools.partial(_conformer_kernel, tm=tm, chunks=chunks),
        grid=(s // tm, chunks),
        in_specs=[pl.BlockSpec((tm, V7X_LANES), lambda i, k: (i, k)),
                  pl.BlockSpec((tm, V7X_LANES), lambda i, k: (i, chunks + k)),
                  pl.BlockSpec((CF_HALO, V7X_LANES),
                               lambda i, k: (jnp.maximum(i * hb - 1, 0), k)),
                  pl.BlockSpec((CF_HALO, V7X_LANES),
                               lambda i, k: (jnp.maximum(i * hb - 1, 0), chunks + k)),
                  pl.BlockSpec((1, CF_KERNEL, V7X_LANES), lambda i, k: (k, 0, 0)),
                  pl.BlockSpec((1, 1, V7X_LANES), lambda i, k: (k, 0, 0)),
                  vec, vec],
        out_specs=pl.BlockSpec((tm, c), lambda i, k: (i, 0)),
        out_shape=jax.ShapeDtypeStruct((s, c), BF16),
        scratch_shapes=[pltpu.VMEM((tm + CF_HALO, V7X_LANES), F32),
                        pltpu.VMEM((chunks, tm, V7X_LANES), F32)],
        compiler_params=_cparams(("parallel", "arbitrary")),
        name="conformer_conv",
    )(hcf, hcf, hcf, hcf, w3, cb3, norm_g, norm_b)


def _diff_qk_kernel(x_ref, w_ref, c_ref, sa_ref, sb_ref, o_ref, *, heads, q_tiles, scale):
    j = pl.program_id(1)
    acc = _bdot(x_ref[...], w_ref[...])
    c, sa, sb = c_ref[...], sa_ref[...], sb_ref[...]
    half = DIFF_ROT_DIM // 2
    mult = jnp.where(j < q_tiles, scale, 1.0)
    for hh in range(heads):
        t = acc[:, hh * V7X_LANES:(hh + 1) * V7X_LANES]
        r = t * c + pltpu.roll(t, V7X_LANES - half, 1) * sa + pltpu.roll(t, half, 1) * sb
        o_ref[hh] = (r * mult).astype(BF16)


def diff_qk(xb, w_qk, c_tab, sa_tab, sb_tab, tm, heads_per_step):
    s, d = xb.shape
    n = w_qk.shape[1]
    tn = heads_per_step * V7X_LANES
    tab = pl.BlockSpec((tm, V7X_LANES), lambda i, j: (i, 0))
    return pl.pallas_call(
        functools.partial(_diff_qk_kernel, heads=heads_per_step, q_tiles=(n // 2) // tn,
                          scale=DIFF_QK_DIM ** -0.5),
        grid=(s // tm, n // tn),
        in_specs=[pl.BlockSpec((tm, d), lambda i, j: (i, 0)),
                  pl.BlockSpec((d, tn), lambda i, j: (0, j)),
                  tab, tab, tab],
        out_specs=pl.BlockSpec((heads_per_step, tm, V7X_LANES), lambda i, j: (j, i, 0)),
        out_shape=jax.ShapeDtypeStruct((n // V7X_LANES, s, V7X_LANES), BF16),
        compiler_params=_cparams(("parallel", "parallel")),
        name="diff_qk",
    )(xb, w_qk, c_tab, sa_tab, sb_tab)


def _out_proj_kernel(a1_ref, a2_ref, w1_ref, w2_ref, x_ref, o_ref):
    acc = _bdot(a1_ref[...], w1_ref[...]) + _bdot(a2_ref[...], w2_ref[...])
    o_ref[...] = acc + DEEPNORM_ALPHA * x_ref[...]


def out_proj(a1, a2, w, x, tm, tn):
    s, k = a1.shape
    n = w.shape[1]
    return pl.pallas_call(
        _out_proj_kernel,
        grid=(s // tm, n // tn),
        in_specs=[pl.BlockSpec((tm, k), lambda i, j: (i, 0)),
                  pl.BlockSpec((tm, k), lambda i, j: (i, 0)),
                  pl.BlockSpec((k, tn), lambda i, j: (0, j)),
                  pl.BlockSpec((k, tn), lambda i, j: (1, j)),
                  pl.BlockSpec((tm, tn), lambda i, j: (i, j))],
        out_specs=pl.BlockSpec((tm, tn), lambda i, j: (i, j)),
        out_shape=jax.ShapeDtypeStruct((s, n), F32),
        compiler_params=_cparams(("parallel", "parallel")),
        name="out_proj",
    )(a1, a2, w, w, x)


def _layer_norm_rows(z, g, b):
    mu = jnp.mean(z, -1, keepdims=True)
    zc = z - mu
    var = jnp.mean(zc * zc, -1, keepdims=True)
    return zc * lax.rsqrt(var + NORM_EPS) * g + b


def _ln_kernel(z_ref, g_ref, b_ref, o_ref, ob_ref):
    y = _layer_norm_rows(z_ref[...], g_ref[...], b_ref[...])
    o_ref[...] = y
    ob_ref[...] = y.astype(BF16)


def layer_norm(z, g, b, tm):
    s, d = z.shape
    vec = pl.BlockSpec((1, d), lambda i: (0, 0))
    row = pl.BlockSpec((tm, d), lambda i: (i, 0))
    return pl.pallas_call(
        _ln_kernel,
        grid=(s // tm,),
        in_specs=[row, vec, vec],
        out_specs=[row, row],
        out_shape=[jax.ShapeDtypeStruct((s, d), F32), jax.ShapeDtypeStruct((s, d), BF16)],
        compiler_params=_cparams(("parallel",)),
        name="layer_norm",
    )(z, g, b)


def _split_bf16(v):
    hi = v.astype(BF16)
    lo = (v - hi.astype(F32)).astype(BF16)
    return hi, lo


def _router_kernel(x_ref, wt_ref, b_ref, idx_ref, gate_ref, rank_ref, cnt_ref, carry_sc, *, tm):
    i = pl.program_id(0)

    @pl.when(i == 0)
    def _():
        carry_sc[...] = jnp.zeros(carry_sc.shape, F32)

    x_hi, x_lo = _split_bf16(x_ref[...])
    w_hi, w_lo = _split_bf16(wt_ref[...])
    logits = _dot_nt(w_hi, x_hi) + (_dot_nt(w_hi, x_lo) + _dot_nt(w_lo, x_hi)) + b_ref[...]

    e_iota = lax.broadcasted_iota(jnp.int32, logits.shape, 0)
    work = logits
    vals, sels = [], []
    for k in range(TOP_K):
        mx = jnp.max(work, axis=0, keepdims=True)
        idx = jnp.min(jnp.where(work == mx, e_iota, N_EXPERTS), axis=0, keepdims=True)
        sel = e_iota == idx
        idx_ref[k:k + 1, :] = idx
        vals.append(mx)
        sels.append(sel)
        work = jnp.where(sel, -jnp.inf, work)

    exps = [jnp.exp(v - vals[0]) for v in vals]
    denom = exps[0] + exps[1] + exps[2] + exps[3]
    for k in range(TOP_K):
        gate_ref[k:k + 1, :] = exps[k] / denom

    onehot = jnp.zeros(logits.shape, F32)
    for sel in sels:
        onehot = onehot + sel.astype(F32)
    r_i = lax.broadcasted_iota(jnp.int32, (tm, tm), 0)
    c_i = lax.broadcasted_iota(jnp.int32, (tm, tm), 1)
    strict_upper = (r_i < c_i).astype(BF16)
    before = carry_sc[...] + _bdot(onehot.astype(BF16), strict_upper)
    for k in range(TOP_K):
        rank = jnp.sum(jnp.where(sels[k], before, 0.0), axis=0, keepdims=True)
        rank_ref[k:k + 1, :] = rank.astype(jnp.int32)
    carry = carry_sc[...] + jnp.sum(onehot, axis=1, keepdims=True)
    carry_sc[...] = carry
    cnt_ref[...] = jnp.broadcast_to(carry, cnt_ref.shape).astype(jnp.int32)


def router(x, wt, bias, tm):
    s, d = x.shape
    e = wt.shape[0]
    tok = pl.BlockSpec((TOP_K, tm), lambda i: (0, i))
    return pl.pallas_call(
        functools.partial(_router_kernel, tm=tm),
        grid=(s // tm,),
        in_specs=[pl.BlockSpec((tm, d), lambda i: (i, 0)),
                  pl.BlockSpec((e, d), lambda i: (0, 0)),
                  pl.BlockSpec((e, 1), lambda i: (0, 0))],
        out_specs=[tok, tok, tok, pl.BlockSpec((e, V7X_LANES), lambda i: (0, 0))],
        out_shape=[jax.ShapeDtypeStruct((TOP_K, s), jnp.int32),
                   jax.ShapeDtypeStruct((TOP_K, s), F32),
                   jax.ShapeDtypeStruct((TOP_K, s), jnp.int32),
                   jax.ShapeDtypeStruct((e, V7X_LANES), jnp.int32)],
        scratch_shapes=[pltpu.VMEM((e, 1), F32)],
        compiler_params=_cparams(("arbitrary",)),
        name="router",
    )(x, wt, bias)


def _row_copy(src_hbm, dst_slot, sem, src_row, dst_row):
    return pltpu.make_async_copy(src_hbm.at[pl.ds(src_row, 1)], dst_slot.at[pl.ds(dst_row, 1)], sem)


def _start_rows(idx_ref, src_hbm, dst_slot, sem, n):
    def body(r, carry):
        _row_copy(src_hbm, dst_slot, sem, idx_ref[0, 0, r], r).start()
        return carry
    lax.fori_loop(0, n, body, 0)


def _wait_rows(src_hbm, dst_slot, sem, n):
    def body(r, carry):
        _row_copy(src_hbm, dst_slot, sem, 0, r).wait()
        return carry
    lax.fori_loop(0, n, body, 0)


def _gathered_rows(idx_ref, idx_next_ref, src_hbm, buf, sems, n):
    i = pl.program_id(0)
    nsteps = pl.num_programs(0)
    slot = lax.rem(i, 2)

    @pl.when(i == 0)
    def _():
        _start_rows(idx_ref, src_hbm, buf.at[0], sems.at[0], n)

    @pl.when(i + 1 < nsteps)
    def _():
        _start_rows(idx_next_ref, src_hbm, buf.at[1 - slot], sems.at[1 - slot], n)

    _wait_rows(src_hbm, buf.at[slot], sems.at[slot], n)
    return slot


def _moe_gather_kernel(idx_ref, idx_next_ref, x_hbm, o_ref, buf, sems, *, rows):
    slot = _gathered_rows(idx_ref, idx_next_ref, x_hbm, buf, sems, rows)
    o_ref[...] = buf[slot].astype(o_ref.dtype)


def moe_gather(x, buf_tok, rows):
    s, d = x.shape
    cap = buf_tok.shape[0]
    nblk = cap // rows
    idx = buf_tok.reshape(nblk, 1, rows)
    return pl.pallas_call(
        functools.partial(_moe_gather_kernel, rows=rows),
        grid=(nblk,),
        in_specs=[pl.BlockSpec((1, 1, rows), lambda b: (b, 0, 0), memory_space=pltpu.SMEM),
                  pl.BlockSpec((1, 1, rows), lambda b: (jnp.minimum(b + 1, nblk - 1), 0, 0),
                               memory_space=pltpu.SMEM),
                  pl.BlockSpec(memory_space=pl.ANY)],
        out_specs=pl.BlockSpec((rows, d), lambda b: (b, 0)),
        out_shape=jax.ShapeDtypeStruct((cap, d), BF16),
        scratch_shapes=[pltpu.VMEM((2, rows, d), F32), pltpu.SemaphoreType.DMA((2,))],
        compiler_params=_cparams(("arbitrary",)),
        name="moe_gather",
    )(idx, idx, x)


def _expert_changed(be_ref, b):
    prev = be_ref[jnp.maximum(b - 1, 0)]
    return jnp.logical_or(b == 0, be_ref[b] != prev)


def _ffn1_kernel(be_ref, nv_ref, xs_ref, wg_ref, wl_ref, bg_ref, bl_ref, o_ref, wg_sc, wl_sc):
    b = pl.program_id(1)

    @pl.when(b < nv_ref[0])
    def _():
        @pl.when(_expert_changed(be_ref, b))
        def _():
            wg_sc[...] = wg_ref[0].astype(BF16)
            wl_sc[...] = wl_ref[0].astype(BF16)

        xs = xs_ref[...]
        glu = _bdot(xs, wg_sc[...]) + bg_ref[0]
        lin = _bdot(xs, wl_sc[...]) + bl_ref[0]
        glu = jnp.minimum(glu, SWIGLU_LIMIT)
        lin = jnp.clip(lin, -SWIGLU_LIMIT, SWIGLU_LIMIT)
        act = glu * jax.nn.sigmoid(SWIGLU_ALPHA * glu) * (lin + 1.0)
        o_ref[...] = act.astype(o_ref.dtype)

    @pl.when(b >= nv_ref[0])
    def _():
        o_ref[...] = jnp.zeros(o_ref.shape, o_ref.dtype)


def moe_ffn1(xs, block_expert, nvalid, w_gu, b_gu, rows, tc):
    cap, d = xs.shape
    e, _, two_f = w_gu.shape
    f = two_f // 2
    nj = f // tc
    nblk = cap // rows
    b_gu3 = b_gu.reshape(e, 1, two_f)

    def row_map(j, b, be, nv):
        return (jnp.minimum(b, nv[0] - 1), 0)

    grid_spec = pltpu.PrefetchScalarGridSpec(
        num_scalar_prefetch=2,
        grid=(nj, nblk),
        in_specs=[pl.BlockSpec((rows, d), row_map),
                  pl.BlockSpec((1, d, tc), lambda j, b, be, nv: (be[b], 0, j)),
                  pl.BlockSpec((1, d, tc), lambda j, b, be, nv: (be[b], 0, nj + j)),
                  pl.BlockSpec((1, 1, tc), lambda j, b, be, nv: (be[b], 0, j)),
                  pl.BlockSpec((1, 1, tc), lambda j, b, be, nv: (be[b], 0, nj + j))],
        out_specs=pl.BlockSpec((rows, tc), lambda j, b, be, nv: (b, j)),
        scratch_shapes=[pltpu.VMEM((d, tc), BF16), pltpu.VMEM((d, tc), BF16)],
    )
    return pl.pallas_call(
        _ffn1_kernel,
        grid_spec=grid_spec,
        out_shape=jax.ShapeDtypeStruct((cap, f), BF16),
        compiler_params=_cparams(("arbitrary", "arbitrary")),
        name="moe_ffn1",
    )(block_expert, nvalid, xs, w_gu, w_gu, b_gu3, b_gu3)


def _ffn2_kernel(be_ref, nv_ref, h_ref, w_ref, bias_ref, o_ref, w_sc):
    b = pl.program_id(1)

    @pl.when(b < nv_ref[0])
    def _():
        @pl.when(_expert_changed(be_ref, b))
        def _():
            w_sc[...] = w_ref[0].astype(BF16)

        o_ref[...] = _bdot(h_ref[...], w_sc[...]) + bias_ref[0]

    @pl.when(b >= nv_ref[0])
    def _():
        o_ref[...] = jnp.zeros(o_ref.shape, o_ref.dtype)


def moe_ffn2(hact, block_expert, nvalid, w_d, b_d, rows, tn):
    cap, f = hact.shape
    e, _, d = w_d.shape
    nblk = cap // rows
    b_d3 = b_d.reshape(e, 1, d)
    grid_spec = pltpu.PrefetchScalarGridSpec(
        num_scalar_prefetch=2,
        grid=(d // tn, nblk),
        in_specs=[pl.BlockSpec((rows, f), lambda j, b, be, nv: (jnp.minimum(b, nv[0] - 1), 0)),
                  pl.BlockSpec((1, f, tn), lambda j, b, be, nv: (be[b], 0, j)),
                  pl.BlockSpec((1, 1, tn), lambda j, b, be, nv: (be[b], 0, j))],
        out_specs=pl.BlockSpec((rows, tn), lambda j, b, be, nv: (b, j)),
        scratch_shapes=[pltpu.VMEM((f, tn), BF16)],
    )
    return pl.pallas_call(
        _ffn2_kernel,
        grid_spec=grid_spec,
        out_shape=jax.ShapeDtypeStruct((cap, d), F32),
        compiler_params=_cparams(("arbitrary", "arbitrary")),
        name="moe_ffn2",
    )(block_expert, nvalid, hact, w_d, b_d3)


def _combine_kernel(idx_ref, idx_next_ref, eo_hbm, x_ref, gate_ref, g_ref, b_ref,
                    o_ref, ob_ref, buf, sems, *, tm):
    slot = _gathered_rows(idx_ref, idx_next_ref, eo_hbm, buf, sems, TOP_K * tm)
    gates = gate_ref[...]
    y = DEEPNORM_ALPHA * x_ref[...]
    for k in range(TOP_K):
        y = y + gates[:, k:k + 1] * buf[slot, k * tm:(k + 1) * tm, :]
    out = _layer_norm_rows(y, g_ref[...], b_ref[...])
    o_ref[...] = out
    ob_ref[...] = out.astype(BF16)


def moe_combine(eo, dest, gates_t, x, g, b, tm):
    s, d = x.shape
    nt = s // tm
    idx = dest.reshape(TOP_K, nt, tm).transpose(1, 0, 2).reshape(nt, 1, TOP_K * tm)
    vec = pl.BlockSpec((1, d), lambda i: (0, 0))
    row = pl.BlockSpec((tm, d), lambda i: (i, 0))
    return pl.pallas_call(
        functools.partial(_combine_kernel, tm=tm),
        grid=(nt,),
        in_specs=[pl.BlockSpec((1, 1, TOP_K * tm), lambda i: (i, 0, 0), memory_space=pltpu.SMEM),
                  pl.BlockSpec((1, 1, TOP_K * tm), lambda i: (jnp.minimum(i + 1, nt - 1), 0, 0),
                               memory_space=pltpu.SMEM),
                  pl.BlockSpec(memory_space=pl.ANY),
                  row,
                  pl.BlockSpec((tm, TOP_K), lambda i: (i, 0)),
                  vec, vec],
        out_specs=[row, row],
        out_shape=[jax.ShapeDtypeStruct((s, d), F32), jax.ShapeDtypeStruct((s, d), BF16)],
        scratch_shapes=[pltpu.VMEM((2, TOP_K * tm, d), F32), pltpu.SemaphoreType.DMA((2,))],
        compiler_params=_cparams(("arbitrary",)),
        name="moe_combine",
    )(idx, idx, eo, x, gates_t, g, b)


def moe_layer(x, router_w, router_b, w_gu, b_gu, w_d, b_d, ln_g, ln_b, tiles):
    s, d = x.shape
    e = router_w.shape[1]
    rows = tiles["moe_rows"]
    top_idx, gates, rank, counts = router(x, router_w.T, router_b.reshape(e, 1), tiles["router_tm"])
    counts = counts[:, 0]
    padded = (counts + rows - 1) // rows * rows
    pend = jnp.cumsum(padded)
    pstart = pend - padded
    dest = pstart[top_idx] + rank
    nblk = (s * TOP_K) // rows + e
    cap = nblk * rows
    tok = jnp.broadcast_to(jnp.arange(s, dtype=jnp.int32)[None, :], dest.shape)
    buf_tok = jnp.zeros((cap,), jnp.int32).at[dest.reshape(-1)].set(
        tok.reshape(-1), unique_indices=True)
    block_expert = jnp.minimum(
        jnp.searchsorted(pend, jnp.arange(nblk, dtype=jnp.int32) * rows, side="right"),
        e - 1).astype(jnp.int32)
    nvalid = (pend[-1:] // rows).astype(jnp.int32)

    xs = moe_gather(x, buf_tok, rows)
    hact = moe_ffn1(xs, block_expert, nvalid, w_gu, b_gu, rows, tiles["ffn1_tc"])
    eo = moe_ffn2(hact, block_expert, nvalid, w_d, b_d, rows, tiles["ffn2_tn"])
    return moe_combine(eo, dest, gates.T, x, ln_g.reshape(1, d), ln_b.reshape(1, d),
                       tiles["combine_tm"])


def _rope_cols(w_pe):
    half = MLA_ROPE_DIM // 2
    x1, x2 = w_pe[..., :half], w_pe[..., half:]
    return jnp.concatenate([x1, x2, -x2, x1], axis=-1)


def _mla_tables(positions):
    half = MLA_ROPE_DIM // 2
    inv_freq = ROPE_THETA ** (-jnp.arange(0, MLA_ROPE_DIM, 2, dtype=F32) / MLA_ROPE_DIM)
    ang = positions.astype(F32)[:, None] * inv_freq
    c, s = jnp.cos(ang), jnp.sin(ang)
    del half
    return jnp.concatenate([c, c, s, s], axis=-1)


def _diff_tables(positions):
    half = DIFF_ROT_DIM // 2
    inv_freq = ROPE_THETA ** (-jnp.arange(0, DIFF_ROT_DIM, 2, dtype=F32) / DIFF_ROT_DIM)
    ang = positions.astype(F32)[:, None] * inv_freq
    c, s = jnp.cos(ang), jnp.sin(ang)
    n = positions.shape[0]
    rest = DIFF_QK_DIM - DIFF_ROT_DIM
    c_tab = jnp.concatenate([c, c, jnp.ones((n, rest), F32)], axis=-1)
    sa_tab = jnp.concatenate([-s, jnp.zeros((n, DIFF_QK_DIM - half), F32)], axis=-1)
    sb_tab = jnp.concatenate([jnp.zeros((n, half), F32), s, jnp.zeros((n, rest), F32)], axis=-1)
    return c_tab, sa_tab, sb_tab


DEFAULT_TILES = dict(
    mm_tm=1024, mm_tn=1024, even_a_tm=512, mla_q_tm=1024, mla_q_heads=4, mla_k_tm=1024,
    mla_k_heads=8, flash_tq=512, flash_tk=512, conv_tm=256, cf_tm=512, diff_qk_heads=8,
    out_tm=1024, out_tn=512, ln_tm=256, router_tm=256, moe_rows=MOE_ROWS, ffn1_tc=256,
    ffn2_tn=1024, combine_tm=128)


def _lambda_init(layer):
    return 0.8 - 0.6 * math.exp(-0.3 * layer)


def even_layer_mixer(x, xb, positions, w_in, q_a_norm, w_q_b, kv_a_norm, w_kv_b, conv_w, w_out, t):
    s, d = x.shape
    cs = _mla_tables(positions)
    qk_cut = Q_LORA_RANK + KV_LORA_RANK
    pe_cut = qk_cut + MLA_ROPE_DIM
    w_a = jnp.concatenate([w_in[:, :qk_cut], _rope_cols(w_in[:, qk_cut:pe_cut])], axis=1).astype(BF16)
    w_sc = w_in[:, pe_cut:].astype(BF16)
    hd_q = MLA_NOPE_DIM + MLA_ROPE_DIM
    wq = w_q_b.reshape(Q_LORA_RANK, MLA_HEADS, hd_q)
    wq = jnp.concatenate([wq[..., :MLA_NOPE_DIM], _rope_cols(wq[..., MLA_NOPE_DIM:])], axis=-1)
    wq = wq.reshape(Q_LORA_RANK, MLA_HEADS * 2 * V7X_LANES).astype(BF16)
    wkv = w_kv_b.reshape(KV_LORA_RANK, MLA_HEADS, MLA_NOPE_DIM + MLA_V_DIM)
    wk = wkv[..., :MLA_NOPE_DIM].reshape(KV_LORA_RANK, MLA_HEADS * MLA_NOPE_DIM).astype(BF16)
    wv = wkv[..., MLA_NOPE_DIM:].reshape(KV_LORA_RANK, MLA_HEADS * MLA_V_DIM).astype(BF16)

    qn, kvn, kpe = even_a(xb, w_a, q_a_norm.reshape(1, -1), kv_a_norm.reshape(1, -1), cs,
                          t["even_a_tm"])
    q_cat = mla_q(qn, wq, cs, t["mla_q_tm"], t["mla_q_heads"])
    k_cat = mla_k(kvn, wk, kpe, t["mla_k_tm"], t["mla_k_heads"])
    v = matmul(kvn, wv, BF16, t["mm_tm"], t["mm_tn"])
    attn = mla_flash(q_cat, k_cat, v, t["flash_tq"], t["flash_tk"])
    hsc = matmul(xb, w_sc, F32, t["mm_tm"], t["mm_tn"])
    conv = short_conv(hsc, conv_w, t["conv_tm"])
    return out_proj(attn, conv, w_out.astype(BF16), x, t["out_tm"], t["out_tn"])


def odd_layer_mixer(x, xb, positions, w_in, cf_conv_w, cf_conv_b, cf_norm_g, cf_norm_b,
                    lq1, lk1, lq2, lk2, subln_g, w_out, lam_init, t):
    qk = DIFF_HEADS * 2 * DIFF_QK_DIM
    v_cut = 2 * qk + DIFF_HEADS * DIFF_V_DIM
    c_tab, sa_tab, sb_tab = _diff_tables(positions)
    w_qk = w_in[:, :2 * qk].astype(BF16)
    w_v = w_in[:, 2 * qk:v_cut].astype(BF16)
    w_cf = w_in[:, v_cut:].astype(BF16)
    qk_heads = diff_qk(xb, w_qk, c_tab, sa_tab, sb_tab, t["mm_tm"], t["diff_qk_heads"])
    v = matmul(xb, w_v, BF16, t["mm_tm"], t["mm_tn"])
    hcf = matmul(xb, w_cf, F32, t["mm_tm"], t["mm_tn"])
    attn = diff_flash(qk_heads, v, lq1.reshape(1, -1), lk1.reshape(1, -1), lq2.reshape(1, -1),
                      lk2.reshape(1, -1), subln_g.reshape(1, -1), lam_init,
                      t["flash_tq"], t["flash_tk"])
    conv = conformer_conv(hcf, cf_conv_w, cf_conv_b.reshape(1, -1), cf_norm_g.reshape(1, -1),
                          cf_norm_b.reshape(1, -1), t["cf_tm"])
    return out_proj(attn, conv, w_out.astype(BF16), x, t["out_tm"], t["out_tn"])


def kernel(x, positions, e_w_in, e_q_a_norm, e_w_q_b, e_kv_a_norm, e_w_kv_b, e_conv_w, e_w_out, o_w_in, o_cf_conv_w, o_cf_conv_b, o_cf_norm_g, o_cf_norm_b, o_lambda_q1, o_lambda_k1, o_lambda_q2, o_lambda_k2, o_subln_g, o_w_out, ln_mix_g, ln_mix_b, router_w, router_b, moe_w_gate_up, moe_b_gate_up, moe_w_down, moe_b_down, ln_ffn_g, ln_ffn_b):
    t = DEFAULT_TILES
    batch, s, d = x.shape
    outs = []
    for bi in range(batch):
        xf = x[bi]
        xb = xf.astype(BF16)
        pos = positions[bi]
        for layer in range(DEPTH):
            i = layer // 2
            if layer % 2 == 0:
                z = even_layer_mixer(xf, xb, pos, e_w_in[i], e_q_a_norm[i], e_w_q_b[i],
                                     e_kv_a_norm[i], e_w_kv_b[i], e_conv_w[i], e_w_out[i], t)
            else:
                z = odd_layer_mixer(xf, xb, pos, o_w_in[i], o_cf_conv_w[i], o_cf_conv_b[i],
                                    o_cf_norm_g[i], o_cf_norm_b[i], o_lambda_q1[i],
                                    o_lambda_k1[i], o_lambda_q2[i], o_lambda_k2[i],
                                    o_subln_g[i], o_w_out[i], _lambda_init(layer), t)
            xf, xb = layer_norm(z, ln_mix_g[layer].reshape(1, d), ln_mix_b[layer].reshape(1, d),
                                t["ln_tm"])
            xf, xb = moe_layer(xf, router_w[layer], router_b[layer], moe_w_gate_up[layer],
                               moe_b_gate_up[layer], moe_w_down[layer], moe_b_down[layer],
                               ln_ffn_g[layer], ln_ffn_b[layer], t)
        outs.append(xf)
    return jnp.stack(outs, axis=0)
```
